```python
import jax, jax.numpy as jnp
from jax import lax
import numpy as np

D_MODEL = 1024
BATCH = 16
SEQ = 2048
DEPTH = 1
DEC_BATCH = 16
DEC_SEQ = 64
PAST_LEN = 2048

CHUNK = 64
D_PLE = 256
D_CONV = 512
CONV_WIDTH = 31
CONV_STATE = CONV_WIDTH - 1
N_HEADS = 8
N_KV_HEADS = 2
HEAD_DIM = 64
GROUP = N_HEADS // N_KV_HEADS
WINDOW = 128
BAND = WINDOW + CHUNK
D_ATTN = N_HEADS * HEAD_DIM
D_KV = N_KV_HEADS * HEAD_DIM
OFF_Q = 2 * D_CONV
OFF_K = OFF_Q + D_ATTN
OFF_V = OFF_K + D_KV
OFF_GATE = OFF_V + D_KV
D_IN = OFF_GATE + 2 * D_MODEL
N_KEYS = 128
N_EXPERTS = N_KEYS * N_KEYS
N_RET_HEADS = 8
D_QUERY = 256
D_HALF = D_QUERY // 2
TOPK_HALF = 16
TOPK = 16
PEER_BLOCK = 128
EPS = 1e-6
NEG_INF = -1e30

kernel_name = 'gated_conformer_swa_peer_stream'


def rms_norm(x, g):
    x32 = x.astype(jnp.float32)
    y = x32 * lax.rsqrt(jnp.mean(x32 * x32, axis=-1, keepdims=True) + EPS)
    return (y * g.astype(jnp.float32)).astype(x.dtype)


def layer_norm(x, g, b):
    x32 = x.astype(jnp.float32)
    mu = jnp.mean(x32, axis=-1, keepdims=True)
    var = jnp.mean(jnp.square(x32 - mu), axis=-1, keepdims=True)
    y = (x32 - mu) * lax.rsqrt(var + EPS) * g.astype(jnp.float32) + b.astype(jnp.float32)
    return y.astype(x.dtype)


def alibi_slopes():
    return jnp.asarray(2.0 ** (-8.0 * np.arange(1, N_HEADS + 1) / N_HEADS), dtype=jnp.float32)


def attend_band(q, k, v, offset, valid, sinks):
    nq, nk = q.shape[-4], k.shape[-3]
    s = jnp.einsum('...qkgd,...skd->...kgqs', q, k,
                   preferred_element_type=jnp.float32) * (HEAD_DIM ** -0.5)
    dist = jnp.abs(jnp.arange(nq)[:, None] + offset - jnp.arange(nk)[None, :]).astype(jnp.float32)
    slopes = alibi_slopes().reshape(N_KV_HEADS, GROUP)
    s = s - slopes[:, :, None, None] * dist
    if valid is not None:
        s = jnp.where(valid[..., None, None, None, :], s, NEG_INF)
    sink = sinks.astype(jnp.float32).reshape(N_KV_HEADS, GROUP)[:, :, None, None]
    lse = jnp.logaddexp(jax.nn.logsumexp(s, axis=-1, keepdims=True), sink)
    p = jnp.exp(s - lse).astype(v.dtype)
    return jnp.einsum('...kgqs,...skd->...qkgd', p, v)


def attn_prompt(q, k, v, sinks):
    B, S = q.shape[0], q.shape[1]
    n_chunks = S // CHUNK
    qb = q.reshape(B, n_chunks, CHUNK, N_KV_HEADS, GROUP, HEAD_DIM)
    kp = jnp.pad(k, ((0, 0), (WINDOW, 0), (0, 0), (0, 0)))
    vp = jnp.pad(v, ((0, 0), (WINDOW, 0), (0, 0), (0, 0)))
    idx = jnp.arange(n_chunks)[:, None] * CHUNK + jnp.arange(BAND)[None, :]
    kb = kp[:, idx]
    vb = vp[:, idx]
    valid = idx >= WINDOW
    o = attend_band(qb, kb, vb, WINDOW, valid, sinks)
    return o.reshape(B, S, D_ATTN)


def attn_sample(q, k, v, past_k, past_v, sinks):
    B, T = q.shape[0], q.shape[1]
    kk = jnp.concatenate([past_k, k], axis=1)
    vv = jnp.concatenate([past_v, v], axis=1)
    qb = q.reshape(B, T, N_KV_HEADS, GROUP, HEAD_DIM)
    o = attend_band(qb, kk, vv, past_k.shape[1], None, sinks)
    return o.reshape(B, T, D_ATTN)


def depthwise_causal(xp, w, b):
    y = lax.conv_general_dilated(xp, w[:, None, :], window_strides=(1,), padding='VALID',
                                 dimension_numbers=('NWC', 'WIO', 'NWC'),
                                 feature_group_count=D_CONV)
    return y + b


def peer(h, w_query, sub_keys1, sub_keys2, expert_u, expert_v):
    T = h.shape[0]
    nb = -(-T // PEER_BLOCK)
    hp = jnp.pad(h, ((0, nb * PEER_BLOCK - T), (0, 0))).reshape(nb, PEER_BLOCK, D_MODEL)

    def block(hb):
        q = (hb @ w_query).reshape(PEER_BLOCK, N_RET_HEADS, D_QUERY)
        s1 = jnp.einsum('trd,nd->trn', q[..., :D_HALF], sub_keys1, preferred_element_type=jnp.float32)
        s2 = jnp.einsum('trd,nd->trn', q[..., D_HALF:], sub_keys2, preferred_element_type=jnp.float32)
        v1, i1 = lax.top_k(s1, TOPK_HALF)
        v2, i2 = lax.top_k(s2, TOPK_HALF)
        cand = (v1[..., :, None] + v2[..., None, :]).reshape(PEER_BLOCK, N_RET_HEADS, TOPK_HALF * TOPK_HALF)
        cidx = (i1[..., :, None] * N_KEYS + i2[..., None, :]).reshape(PEER_BLOCK, N_RET_HEADS, TOPK_HALF * TOPK_HALF)
        sv, si = lax.top_k(cand, TOPK)
        eidx = jnp.take_along_axis(cidx, si, axis=-1)
        g = jax.nn.softmax(sv, axis=-1)
        ue = jnp.take(expert_u, eidx, axis=0)
        a = jnp.einsum('trkd,td->trk', ue, hb, preferred_element_type=jnp.float32)
        act = (jax.nn.gelu(a) * g).astype(hb.dtype)
        ve = jnp.take(expert_v, eidx, axis=0)
        return jnp.einsum('trk,trkd->td', act, ve)

    out = lax.map(block, hp).reshape(nb * PEER_BLOCK, D_MODEL)
    return out[:T]


def trunk_layer(x, p, past_k, past_v, past_conv,
                g_mix, w_in, b_in, w_dw, b_dw, g_cln, b_cln, w_cout, b_cout, sinks, w_aout, w_out,
                g_ffn, w_query, sub_keys1, sub_keys2, expert_u, expert_v, w_ple, g_ple, w_ple_gate):
    B, T, _ = x.shape
    h = rms_norm(x, g_mix)
    z = h @ w_in + b_in
    glu = z[..., :D_CONV] * jax.nn.sigmoid(z[..., D_CONV:OFF_Q])
    q = z[..., OFF_Q:OFF_K].reshape(B, T, N_HEADS, HEAD_DIM)
    k = z[..., OFF_K:OFF_V].reshape(B, T, N_KV_HEADS, HEAD_DIM)
    v = z[..., OFF_V:OFF_GATE].reshape(B, T, N_KV_HEADS, HEAD_DIM)
    gates = jax.nn.sigmoid(z[..., OFF_GATE:])
    if past_conv is None:
        conv_in = jnp.pad(glu, ((0, 0), (CONV_STATE, 0), (0, 0)))
        att = attn_prompt(q, k, v, sinks)
        new_k = k[:, -WINDOW:]
        new_v = v[:, -WINDOW:]
    else:
        conv_in = jnp.concatenate([past_conv, glu], axis=1)
        att = attn_sample(q, k, v, past_k, past_v, sinks)
        n_keep = past_k.shape[1]
        new_k = jnp.concatenate([past_k, k], axis=1)[:, -n_keep:]
        new_v = jnp.concatenate([past_v, v], axis=1)[:, -n_keep:]
    new_conv = conv_in[:, -CONV_STATE:]
    c = jax.nn.silu(layer_norm(depthwise_causal(conv_in, w_dw, b_dw), g_cln, b_cln))
    branch_a = c @ w_cout + b_cout
    branch_b = att @ w_aout
    merged = gates[..., :D_MODEL] * branch_a + gates[..., D_MODEL:] * branch_b
    x = x + merged @ w_out
    h2 = rms_norm(x, g_ffn)
    x = x + peer(h2.reshape(B * T, D_MODEL), w_query, sub_keys1, sub_keys2,
                 expert_u, expert_v).reshape(B, T, D_MODEL)
    ple_gate = jax.nn.sigmoid(rms_norm(x, g_ple) @ w_ple_gate)
    x = x + ple_gate * (p @ w_ple)
    return x, new_k, new_v, new_conv


def setup_inputs(seed: int = 0) -> dict:
    key = jax.random.key(seed)
    ks = jax.random.split(key, 32)
    f32 = jnp.float32

    def nrm(k, shape, scale):
        return jax.random.normal(k, shape, f32) * scale

    def gain(k, shape):
        return 1.0 + 0.1 * jax.random.normal(k, shape, f32)

    kv_cache = min(WINDOW, PAST_LEN)
    return {
        'x_prompt': nrm(ks[0], (BATCH, SEQ, D_MODEL), 1.0),
        'x_sample': nrm(ks[1], (DEC_BATCH, DEC_SEQ, D_MODEL), 1.0),
        'p_prompt': nrm(ks[2], (DEPTH, BATCH, SEQ, D_PLE), 1.0),
        'p_sample': nrm(ks[3], (DEPTH, DEC_BATCH, DEC_SEQ, D_PLE), 1.0),
        'cache_k': nrm(ks[4], (DEPTH, DEC_BATCH, kv_cache, N_KV_HEADS, HEAD_DIM), 1.0),
        'cache_v': nrm(ks[5], (DEPTH, DEC_BATCH, kv_cache, N_KV_HEADS, HEAD_DIM), 1.0),
        'state_conv': nrm(ks[6], (DEPTH, DEC_BATCH, CONV_STATE, D_CONV), 0.5),
        'g_mix': gain(ks[7], (DEPTH, D_MODEL)),
        'w_in': nrm(ks[8], (DEPTH, D_MODEL, D_IN), D_MODEL ** -0.5),
        'b_in': nrm(ks[9], (DEPTH, D_IN), 0.02),
        'w_dw': nrm(ks[10], (DEPTH, CONV_WIDTH, D_CONV), CONV_WIDTH ** -0.5),
        'b_dw': nrm(ks[11], (DEPTH, D_CONV), 0.02),
        'g_cln': gain(ks[12], (DEPTH, D_CONV)),
        'b_cln': nrm(ks[13], (DEPTH, D_CONV), 0.02),
        'w_cout': nrm(ks[14], (DEPTH, D_CONV, D_MODEL), D_CONV ** -0.5),
        'b_cout': nrm(ks[15], (DEPTH, D_MODEL), 0.02),
        'sinks': nrm(ks[16], (DEPTH, N_HEADS), 1.0),
        'w_aout': nrm(ks[17], (DEPTH, D_ATTN, D_MODEL), D_ATTN ** -0.5),
        'w_out': nrm(ks[18], (DEPTH, D_MODEL, D_MODEL), D_MODEL ** -0.5),
        'g_ffn': gain(ks[19], (DEPTH, D_MODEL)),
        'w_query': nrm(ks[20], (DEPTH, D_MODEL, N_RET_HEADS * D_QUERY), D_MODEL ** -0.5),
        'sub_keys1': nrm(ks[21], (DEPTH, N_KEYS, D_HALF), D_HALF ** -0.5),
        'sub_keys2': nrm(ks[22], (DEPTH, N_KEYS, D_HALF), D_HALF ** -0.5),
        'expert_u': nrm(ks[23], (DEPTH, N_EXPERTS, D_MODEL), D_MODEL ** -0.5),
        'expert_v': nrm(ks[24], (DEPTH, N_EXPERTS, D_MODEL), 0.5),
        'w_ple': nrm(ks[25], (DEPTH, D_PLE, D_MODEL), D_PLE ** -0.5),
        'g_ple': gain(ks[26], (DEPTH, D_MODEL)),
        'w_ple_gate': nrm(ks[27], (DEPTH, D_MODEL, D_MODEL), D_MODEL ** -0.5),
        'g_final': gain(ks[28], (D_MODEL,)),
    }


def reference(x_prompt, x_sample, p_prompt, p_sample, cache_k, cache_v, state_conv,
              g_mix, w_in, b_in, w_dw, b_dw, g_cln, b_cln, w_cout, b_cout, sinks, w_aout, w_out,
              g_ffn, w_query, sub_keys1, sub_keys2, expert_u, expert_v, w_ple, g_ple, w_ple_gate,
              g_final):
    xp, xs = x_prompt, x_sample
    kp_l, vp_l, cp_l, ks_l, vs_l, cs_l = [], [], [], [], [], []
    for i in range(DEPTH):
        w = (g_mix[i], w_in[i], b_in[i], w_dw[i], b_dw[i], g_cln[i], b_cln[i], w_cout[i], b_cout[i],
             sinks[i], w_aout[i], w_out[i], g_ffn[i], w_query[i], sub_keys1[i], sub_keys2[i],
             expert_u[i], expert_v[i], w_ple[i], g_ple[i], w_ple_gate[i])
        xp, nk, nv, nc = trunk_layer(xp, p_prompt[i], None, None, None, *w)
        kp_l.append(nk); vp_l.append(nv); cp_l.append(nc)
        xs, nk, nv, nc = trunk_layer(xs, p_sample[i], cache_k[i], cache_v[i], state_conv[i], *w)
        ks_l.append(nk); vs_l.append(nv); cs_l.append(nc)
    y_prompt = rms_norm(xp, g_final)
    y_sample = rms_norm(xs, g_final)
    return (y_prompt, y_sample,
            jnp.stack(kp_l), jnp.stack(vp_l), jnp.stack(cp_l),
            jnp.stack(ks_l), jnp.stack(vs_l), jnp.stack(cs_l))
```

```python
import functools

import numpy as np
import jax
import jax.numpy as jnp
from jax import lax
from jax.experimental import pallas as pl
from jax.experimental.pallas import tpu as pltpu

D_MODEL = 1024
D_PLE = 256
D_CONV = 512
CONV_WIDTH = 31
CONV_STATE = CONV_WIDTH - 1
CONV_PAD = 32
N_HEADS = 8
N_KV_HEADS = 2
HEAD_DIM = 64
WINDOW = 128
CHUNK = 64
D_ATTN = N_HEADS * HEAD_DIM
D_KV = N_KV_HEADS * HEAD_DIM
OFF_Q = 2 * D_CONV
OFF_K = OFF_Q + D_ATTN
OFF_V = OFF_K + D_KV
OFF_GATE = OFF_V + D_KV
D_IN = OFF_GATE + 2 * D_MODEL
N_KEYS = 128
N_EXPERTS = N_KEYS * N_KEYS
N_RET_HEADS = 8
D_QUERY = 256
D_HALF = D_QUERY // 2
TOPK = 16
EPS = 1e-6
NEG_INF = -1e30

LANES = 128
SUBLANES = 8
BF16_ROWS = 16
VMEM_LIMIT = 56 * 1024 * 1024

BF16 = jnp.bfloat16
F32 = jnp.float32


def _dot(a, b):
    return jnp.dot(a, b, preferred_element_type=F32)


def _dot_nt(a, b):
    return lax.dot_general(a, b, (((1,), (1,)), ((), ())), preferred_element_type=F32)


def _rms(x, g):
    return x * lax.rsqrt(jnp.mean(x * x, axis=-1, keepdims=True) + EPS) * g


def _attn_bias_tables(tq, first):
    band = WINDOW + tq
    i = np.arange(tq)[:, None]
    j = np.arange(band)[None, :]
    dist = np.abs(i + WINDOW - j).astype(np.float32)
    qi, kj = i // CHUNK, j // CHUNK
    valid = (kj >= qi) & (kj <= qi + 2)
    if first:
        valid = valid & (kj >= 2)
    slopes = (2.0 ** (-8.0 * np.arange(1, N_HEADS + 1) / N_HEADS)).astype(np.float32)
    out = np.zeros((4, 2 * tq, band), np.float32)
    for g in range(N_KV_HEADS):
        for par in range(2):
            for half, h in enumerate((4 * g + par, 4 * g + 2 + par)):
                out[2 * g + par, half * tq:(half + 1) * tq] = np.where(valid, -slopes[h] * dist, NEG_INF)
    return out


def _mixer_kernel(sinks_ref, x_ref, kinit_ref, vinit_ref, ginit_ref, biasf_ref, biasg_ref,
                  gmix_ref, win_ref, bin_ref, wdw_ref, bdw_ref, gcln_ref, bcln_ref,
                  wcout_ref, bcout_ref, waout_ref, wout_ref,
                  x1_ref, knew_ref, vnew_ref, gnew_ref,
                  kext, vext, gext, att, *, ts, tq):
    i = pl.program_id(1)
    band = WINDOW + tq

    @pl.when(i == 0)
    def _():
        kext[0:WINDOW, :] = kinit_ref[...]
        vext[0:WINDOW, :] = vinit_ref[...]
        gext[0:CONV_PAD, :] = ginit_ref[...]

    x = x_ref[...]
    hb = _rms(x, gmix_ref[...]).astype(BF16)

    zg = _dot(hb, win_ref[:, 0:OFF_Q]) + bin_ref[:, 0:OFF_Q]
    glu = zg[:, 0:D_CONV] * jax.nn.sigmoid(zg[:, D_CONV:OFF_Q])
    gext[CONV_PAD:CONV_PAD + ts, :] = glu
    conv = jnp.broadcast_to(bdw_ref[...], (ts, D_CONV))
    for j in range(CONV_WIDTH):
        conv = conv + wdw_ref[j:j + 1, :] * gext[CONV_PAD - CONV_STATE + j:CONV_PAD - CONV_STATE + j + ts, :]
    tail = gext[ts:ts + CONV_PAD, :]
    gnew_ref[...] = tail
    gext[0:CONV_PAD, :] = tail
    mu = jnp.mean(conv, axis=-1, keepdims=True)
    cen = conv - mu
    var = jnp.mean(cen * cen, axis=-1, keepdims=True)
    c = cen * lax.rsqrt(var + EPS) * gcln_ref[...] + bcln_ref[...]
    c = c * jax.nn.sigmoid(c)
    branch_a = _dot(c.astype(BF16), wcout_ref[...]) + bcout_ref[...]

    zqkv = _dot(hb, win_ref[:, OFF_Q:OFF_GATE]) + bin_ref[:, OFF_Q:OFF_GATE]
    kext[WINDOW:WINDOW + ts, :] = zqkv[:, D_ATTN:D_ATTN + D_KV]
    vext[WINDOW:WINDOW + ts, :] = zqkv[:, D_ATTN + D_KV:D_ATTN + 2 * D_KV]
    lane = lax.broadcasted_iota(jnp.int32, (band, D_KV), 1)
    lo = lane < HEAD_DIM
    row = lax.broadcasted_iota(jnp.int32, (2 * tq, 1), 0)
    top = row < tq
    for qb in range(ts // tq):
        kb = kext[qb * tq:qb * tq + band, :]
        vb = vext[qb * tq:qb * tq + band, :]
        kr = pltpu.roll(kb, HEAD_DIM, 1)
        vr = pltpu.roll(vb, HEAD_DIM, 1)
        for g in range(N_KV_HEADS):
            ksrc = (kb, kr) if g == 0 else (kr, kb)
            vsrc = (vb, vr) if g == 0 else (vr, vb)
            qst = jnp.concatenate(
                [zqkv[qb * tq:(qb + 1) * tq, 256 * g:256 * g + LANES],
                 zqkv[qb * tq:(qb + 1) * tq, 256 * g + LANES:256 * g + 2 * LANES]], axis=0).astype(BF16)
            o = None
            for par in range(2):
                keep = lo if par == 0 else jnp.logical_not(lo)
                kp = jnp.where(keep, ksrc[par], 0.0).astype(BF16)
                vp = jnp.where(keep, vsrc[par], 0.0).astype(BF16)
                if qb == 0:
                    bias = jnp.where(i == 0, biasf_ref[2 * g + par], biasg_ref[2 * g + par])
                else:
                    bias = biasg_ref[2 * g + par]
                s = _dot_nt(qst, kp) * (HEAD_DIM ** -0.5) + bias
                sink = jnp.where(top, sinks_ref[4 * g + par], sinks_ref[4 * g + 2 + par])
                mx = jnp.maximum(jnp.max(s, axis=-1, keepdims=True), sink)
                p = jnp.exp(s - mx)
                den = jnp.sum(p, axis=-1, keepdims=True) + jnp.exp(sink - mx)
                p = (p / den).astype(BF16)
                pv = _dot(p, vp)
                o = pv if o is None else o + pv
            att[qb * tq:(qb + 1) * tq, 256 * g:256 * g + LANES] = o[0:tq]
            att[qb * tq:(qb + 1) * tq, 256 * g + LANES:256 * g + 2 * LANES] = o[tq:2 * tq]
    ktail = kext[ts:ts + WINDOW, :]
    vtail = vext[ts:ts + WINDOW, :]
    knew_ref[...] = ktail
    vnew_ref[...] = vtail
    kext[0:WINDOW, :] = ktail
    vext[0:WINDOW, :] = vtail
    branch_b = _dot(att[...].astype(BF16), waout_ref[...])

    zgate = _dot(hb, win_ref[:, OFF_GATE:D_IN]) + bin_ref[:, OFF_GATE:D_IN]
    gates = jax.nn.sigmoid(zgate)
    merged = gates[:, 0:D_MODEL] * branch_a + gates[:, D_MODEL:2 * D_MODEL] * branch_b
    x1_ref[...] = x + _dot(merged.astype(BF16), wout_ref[...])


def _const_spec(shape):
    nd = len(shape)
    return pl.BlockSpec(shape, lambda *_: (0,) * nd)


def _mixer(x, kinit, vinit, ginit, sinks, w, *, ts, tq, has_past):
    bsz, seq, _ = x.shape
    nblk = seq // ts
    band = WINDOW + tq
    biasf = jnp.asarray(_attn_bias_tables(tq, not has_past))
    biasg = jnp.asarray(_attn_bias_tables(tq, False))
    per_b = lambda r, c: pl.BlockSpec((None, r, c), lambda b, i: (b, 0, 0))
    in_specs = [
        pl.BlockSpec(memory_space=pltpu.SMEM),
        pl.BlockSpec((None, ts, D_MODEL), lambda b, i: (b, i, 0)),
        per_b(WINDOW, D_KV), per_b(WINDOW, D_KV), per_b(CONV_PAD, D_CONV),
        _const_spec((4, 2 * tq, band)), _const_spec((4, 2 * tq, band)),
        _const_spec((1, D_MODEL)), _const_spec((D_MODEL, D_IN)), _const_spec((1, D_IN)),
        _const_spec((CONV_PAD, D_CONV)), _const_spec((1, D_CONV)), _const_spec((1, D_CONV)),
        _const_spec((1, D_CONV)), _const_spec((D_CONV, D_MODEL)), _const_spec((1, D_MODEL)),
        _const_spec((D_ATTN, D_MODEL)), _const_spec((D_MODEL, D_MODEL)),
    ]
    out_specs = [
        pl.BlockSpec((None, ts, D_MODEL), lambda b, i: (b, i, 0)),
        per_b(WINDOW, D_KV), per_b(WINDOW, D_KV), per_b(CONV_PAD, D_CONV),
    ]
    out_shape = [
        jax.ShapeDtypeStruct((bsz, seq, D_MODEL), F32),
        jax.ShapeDtypeStruct((bsz, WINDOW, D_KV), F32),
        jax.ShapeDtypeStruct((bsz, WINDOW, D_KV), F32),
        jax.ShapeDtypeStruct((bsz, CONV_PAD, D_CONV), F32),
    ]
    scratch = [
        pltpu.VMEM((WINDOW + ts, D_KV), F32),
        pltpu.VMEM((WINDOW + ts, D_KV), F32),
        pltpu.VMEM((CONV_PAD + ts, D_CONV), F32),
        pltpu.VMEM((ts, D_ATTN), F32),
    ]
    return pl.pallas_call(
        functools.partial(_mixer_kernel, ts=ts, tq=tq),
        grid=(bsz, nblk),
        in_specs=in_specs, out_specs=out_specs, out_shape=out_shape, scratch_shapes=scratch,
        compiler_params=pltpu.CompilerParams(
            dimension_semantics=("arbitrary", "arbitrary"), vmem_limit_bytes=VMEM_LIMIT),
        name=f"mixer_ts{ts}",
    )(sinks, x, kinit, vinit, ginit, biasf, biasg,
      w["g_mix"], w["w_in"], w["b_in"], w["w_dw"], w["b_dw"], w["g_cln"], w["b_cln"],
      w["w_cout"], w["b_cout"], w["w_aout"], w["w_out"])


def _oddeven_merge_sort_pairs(n):
    pairs = []

    def merge(lo, hi, r):
        step = r * 2
        if step < hi - lo:
            merge(lo, hi, step)
            merge(lo + r, hi, step)
            for i in range(lo + r, hi - r, step):
                pairs.append((i, i + r))
        else:
            pairs.append((lo, lo + r))

    def sort(lo, hi):
        if hi - lo >= 1:
            mid = lo + (hi - lo) // 2
            sort(lo, mid)
            sort(mid + 1, hi)
            merge(lo, hi, 1)

    sort(0, n - 1)
    return pairs


def _bitonic_merge_pairs(n):
    pairs = []
    s = n // 2
    while s >= 1:
        for i in range(n):
            if (i // s) % 2 == 0:
                pairs.append((i, i + s))
        s //= 2
    return pairs


_SORT16 = _oddeven_merge_sort_pairs(TOPK)
_BITONIC16 = _bitonic_merge_pairs(TOPK)


def _apply_network(xs, pairs):
    xs = list(xs)
    for a, b in pairs:
        hi = jnp.maximum(xs[a], xs[b])
        lo = jnp.minimum(xs[a], xs[b])
        xs[a], xs[b] = hi, lo
    return xs


def _top16_of_two_sorted(xs, ys, sort_result=True):
    zs = [jnp.maximum(xs[k], ys[TOPK - 1 - k]) for k in range(TOPK)]
    return _apply_network(zs, _BITONIC16) if sort_result else zs


_INNER_PAIRS = [(k1, k2) for k1 in range(1, TOPK) for k2 in range(1, TOPK) if (k1 + 1) * (k2 + 1) <= TOPK]


def _peer_prep_kernel(x1_ref, gffn_ref, wq_ref, sk1_ref, sk2_ref,
                      h2_ref, s1_ref, c1_ref, s2_ref, e2_ref, tau_ref,
                      vt1, vt2, m1row, m2row, zrow, *, tb):
    ngrp = tb // LANES
    h2 = _rms(x1_ref[...], gffn_ref[...]).astype(BF16)
    h2_ref[...] = h2
    q = _dot(h2, wq_ref[...]).astype(BF16)
    sk1 = sk1_ref[...]
    sk2 = sk2_ref[...]
    for r in range(N_RET_HEADS):
        s1_ref[r] = _dot_nt(sk1, q[:, D_QUERY * r:D_QUERY * r + D_HALF])
        s2_ref[r] = _dot_nt(sk2, q[:, D_QUERY * r + D_HALF:D_QUERY * (r + 1)])

    def top16_rows(s_ref, vt, r):
        for lg in range(ngrp):
            tile = s_ref[r, :, lg * LANES:(lg + 1) * LANES]
            xs = [tile[SUBLANES * j:SUBLANES * (j + 1), :] for j in range(N_KEYS // SUBLANES)]
            xs = _apply_network(xs, _SORT16)
            for shift in (4, 2, 1):
                ys = [pltpu.roll(v, shift, 0) for v in xs]
                xs = _top16_of_two_sorted(xs, ys)
            for k in range(TOPK):
                vt[r, k, lg:lg + 1, :] = xs[k][0:1, :]

    def body(r, carry):
        top16_rows(s1_ref, vt1, r)
        top16_rows(s2_ref, vt2, r)
        return carry

    lax.fori_loop(0, N_RET_HEADS, body, 0)

    ninf = jnp.full((ngrp, LANES), -jnp.inf, F32)
    for r in range(N_RET_HEADS):
        a = [vt1[r, k] for k in range(TOPK)]
        b = [vt2[r, k] for k in range(TOPK)]
        row0 = [a[0] + b[k] for k in range(TOPK)]
        col0 = [a[k] + b[0] for k in range(1, TOPK)] + [ninf]
        inner = [a[k1] + b[k2] for (k1, k2) in _INNER_PAIRS]
        inner = inner + [ninf] * (2 * TOPK - len(inner))
        ab = _top16_of_two_sorted(row0, col0)
        c0 = _apply_network(inner[0:TOPK], _SORT16)
        c1 = _apply_network(inner[TOPK:2 * TOPK], _SORT16)
        cc = _top16_of_two_sorted(c0, c1)
        best = _top16_of_two_sorted(ab, cc, sort_result=False)
        m = row0[0]
        tau = best[0]
        z = jnp.exp(best[0] - m)
        for k in range(1, TOPK):
            tau = jnp.minimum(tau, best[k])
            z = z + jnp.exp(best[k] - m)
        zinv = 1.0 / z
        for lg in range(ngrp):
            sl = slice(lg * LANES, (lg + 1) * LANES)
            tau_ref[r:r + 1, sl] = tau[lg:lg + 1, :]
            zrow[r:r + 1, sl] = zinv[lg:lg + 1, :]
            m1row[r:r + 1, sl] = a[0][lg:lg + 1, :]
            m2row[r:r + 1, sl] = b[0][lg:lg + 1, :]

    for r in range(N_RET_HEADS):
        c1_ref[r] = jnp.exp(s1_ref[r] - m1row[r:r + 1, :]) * zrow[r:r + 1, :]
        e2_ref[r] = jnp.exp(s2_ref[r] - m2row[r:r + 1, :])


def _peer_prep(x1, g_ffn, wq, sk1, sk2, *, tb):
    t = x1.shape[0]
    ngrp = tb // LANES
    sc_spec = pl.BlockSpec((N_RET_HEADS, N_KEYS, tb), lambda i: (0, 0, i))
    sc_shape = jax.ShapeDtypeStruct((N_RET_HEADS, N_KEYS, t), F32)
    return pl.pallas_call(
        functools.partial(_peer_prep_kernel, tb=tb),
        grid=(t // tb,),
        in_specs=[
            pl.BlockSpec((tb, D_MODEL), lambda i: (i, 0)),
            _const_spec((1, D_MODEL)),
            _const_spec((D_MODEL, N_RET_HEADS * D_QUERY)),
            _const_spec((N_KEYS, D_HALF)), _const_spec((N_KEYS, D_HALF)),
        ],
        out_specs=[
            pl.BlockSpec((tb, D_MODEL), lambda i: (i, 0)),
            sc_spec, sc_spec, sc_spec, sc_spec,
            pl.BlockSpec((N_RET_HEADS, tb), lambda i: (0, i)),
        ],
        out_shape=[
            jax.ShapeDtypeStruct((t, D_MODEL), BF16),
            sc_shape, sc_shape, sc_shape, sc_shape,
            jax.ShapeDtypeStruct((N_RET_HEADS, t), F32),
        ],
        scratch_shapes=[
            pltpu.VMEM((N_RET_HEADS, TOPK, ngrp, LANES), F32),
            pltpu.VMEM((N_RET_HEADS, TOPK, ngrp, LANES), F32),
            pltpu.VMEM((N_RET_HEADS, tb), F32),
            pltpu.VMEM((N_RET_HEADS, tb), F32),
            pltpu.VMEM((N_RET_HEADS, tb), F32),
        ],
        compiler_params=pltpu.CompilerParams(
            dimension_semantics=("arbitrary",), vmem_limit_bytes=VMEM_LIMIT),
        name=f"peer_prep_t{t}",
    )(x1, g_ffn, wq, sk1, sk2)


def _peer_dense_kernel(h2_ref, u_ref, vt_ref, s1_ref, c1_ref, s2_ref, e2_ref, tau_ref,
                       out_ref, acc, at, pmat, *, tb, ec, lw):
    j = pl.program_id(1)
    nrow = ec // N_KEYS

    @pl.when(j == 0)
    def _():
        acc[...] = jnp.zeros_like(acc)

    at[...] = _dot_nt(u_ref[...], h2_ref[...])

    for il in range(nrow):
        i1 = j * nrow + il
        for lb in range(tb // lw):
            ls = slice(lb * lw, (lb + 1) * lw)
            s1b, c1b, taub = [], [], []
            for r in range(N_RET_HEADS):
                s1b.append(jnp.broadcast_to(s1_ref[r, pl.ds(i1, 1), ls], (BF16_ROWS, lw)))
                c1b.append(jnp.broadcast_to(c1_ref[r, pl.ds(i1, 1), ls], (BF16_ROWS, lw)))
                taub.append(jnp.broadcast_to(tau_ref[r:r + 1, ls], (BF16_ROWS, lw)))

            def rows(rg, carry):
                r0 = pl.multiple_of(rg * BF16_ROWS, BF16_ROWS)
                w = jnp.zeros((BF16_ROWS, lw), F32)
                for r in range(N_RET_HEADS):
                    s = s1b[r] + s2_ref[r, pl.ds(r0, BF16_ROWS), ls]
                    w = w + jnp.where(s >= taub[r], e2_ref[r, pl.ds(r0, BF16_ROWS), ls], 0.0) * c1b[r]
                a = at[pl.ds(il * N_KEYS + r0, BF16_ROWS), ls]
                pmat[pl.ds(il * N_KEYS + r0, BF16_ROWS), ls] = (jax.nn.gelu(a) * w).astype(BF16)
                return carry

            lax.fori_loop(0, N_KEYS // BF16_ROWS, rows, 0)

    acc[...] += _dot(vt_ref[...], pmat[...])

    @pl.when(j == pl.num_programs(1) - 1)
    def _():
        out_ref[...] = acc[...].T


def _peer_dense(h2, u, vt, s1, c1, s2, e2, tau, *, tb, ec, lw):
    t = h2.shape[0]
    sc_spec = pl.BlockSpec((N_RET_HEADS, N_KEYS, tb), lambda i, j: (0, 0, i))
    return pl.pallas_call(
        functools.partial(_peer_dense_kernel, tb=tb, ec=ec, lw=lw),
        grid=(t // tb, N_EXPERTS // ec),
        in_specs=[
            pl.BlockSpec((tb, D_MODEL), lambda i, j: (i, 0)),
            pl.BlockSpec((ec, D_MODEL), lambda i, j: (j, 0)),
            pl.BlockSpec((D_MODEL, ec), lambda i, j: (0, j)),
            sc_spec, sc_spec, sc_spec, sc_spec,
            pl.BlockSpec((N_RET_HEADS, tb), lambda i, j: (0, i)),
        ],
        out_specs=pl.BlockSpec((tb, D_MODEL), lambda i, j: (i, 0)),
        out_shape=jax.ShapeDtypeStruct((t, D_MODEL), F32),
        scratch_shapes=[
            pltpu.VMEM((D_MODEL, tb), F32),
            pltpu.VMEM((ec, tb), F32),
            pltpu.VMEM((ec, tb), BF16),
        ],
        compiler_params=pltpu.CompilerParams(
            dimension_semantics=("arbitrary", "arbitrary"), vmem_limit_bytes=VMEM_LIMIT),
        name=f"peer_dense_t{t}",
    )(h2, u, vt, s1, c1, s2, e2, tau)


def _ple_kernel(x1_ref, peer_ref, p_ref, gple_ref, wgate_ref, wple_ref, gfin_ref, y_ref):
    x2 = x1_ref[...] + peer_ref[...]
    gate = jax.nn.sigmoid(_dot(_rms(x2, gple_ref[...]).astype(BF16), wgate_ref[...]))
    x3 = x2 + gate * _dot(p_ref[...].astype(BF16), wple_ref[...])
    y_ref[...] = _rms(x3, gfin_ref[...])


def _ple(x1, peer, p, g_ple, wgate, wple, g_final, *, tm):
    t = x1.shape[0]
    row = lambda c: pl.BlockSpec((tm, c), lambda i: (i, 0))
    return pl.pallas_call(
        _ple_kernel,
        grid=(t // tm,),
        in_specs=[row(D_MODEL), row(D_MODEL), row(D_PLE), _const_spec((1, D_MODEL)),
                  _const_spec((D_MODEL, D_MODEL)), _const_spec((D_PLE, D_MODEL)), _const_spec((1, D_MODEL))],
        out_specs=row(D_MODEL),
        out_shape=jax.ShapeDtypeStruct((t, D_MODEL), F32),
        compiler_params=pltpu.CompilerParams(
            dimension_semantics=("arbitrary",), vmem_limit_bytes=VMEM_LIMIT),
        name=f"ple_t{t}",
    )(x1, peer, p, g_ple, wgate, wple, g_final)


def _layer(x, p, kinit, vinit, ginit, w, g_final, *, ts, tq, tb, ec, lw, tm, has_past):
    bsz, seq, _ = x.shape
    x1, knew, vnew, gnew = _mixer(x, kinit, vinit, ginit, w["sinks"], w, ts=ts, tq=tq, has_past=has_past)
    x1f = x1.reshape(bsz * seq, D_MODEL)
    h2, s1, c1, s2, e2, tau = _peer_prep(x1f, w["g_ffn"], w["w_query"], w["sub_keys1"], w["sub_keys2"], tb=tb)
    peer = _peer_dense(h2, w["expert_u"], w["expert_vt"], s1, c1, s2, e2, tau, tb=tb, ec=ec, lw=lw)
    y = _ple(x1f, peer, p.reshape(bsz * seq, D_PLE), w["g_ple"], w["w_ple_gate"], w["w_ple"], g_final, tm=tm)
    return y.reshape(bsz, seq, D_MODEL), knew, vnew, gnew


def kernel(x_prompt, x_sample, p_prompt, p_sample, cache_k, cache_v, state_conv, g_mix, w_in, b_in, w_dw, b_dw, g_cln, b_cln, w_cout, b_cout, sinks, w_aout, w_out, g_ffn, w_query, sub_keys1, sub_keys2, expert_u, expert_v, w_ple, g_ple, w_ple_gate, g_final):
    depth = g_mix.shape[0]
    assert depth == 1
    bsz, seq, _ = x_prompt.shape
    dbsz, dseq, _ = x_sample.shape
    n_cache = cache_k.shape[2]
    assert n_cache == WINDOW and dseq == CHUNK and seq % 512 == 0

    row = lambda a: a.reshape(1, -1)
    w = dict(
        g_mix=row(g_mix[0]), w_in=w_in[0].astype(BF16), b_in=row(b_in[0]),
        w_dw=jnp.pad(w_dw[0], ((0, CONV_PAD - CONV_WIDTH), (0, 0))), b_dw=row(b_dw[0]),
        g_cln=row(g_cln[0]), b_cln=row(b_cln[0]),
        w_cout=w_cout[0].astype(BF16), b_cout=row(b_cout[0]), sinks=sinks[0],
        w_aout=w_aout[0].astype(BF16), w_out=w_out[0].astype(BF16),
        g_ffn=row(g_ffn[0]), w_query=w_query[0].astype(BF16),
        sub_keys1=sub_keys1[0].astype(BF16), sub_keys2=sub_keys2[0].astype(BF16),
        expert_u=expert_u[0].astype(BF16), expert_vt=expert_v[0].astype(BF16).T,
        w_ple=w_ple[0].astype(BF16), g_ple=row(g_ple[0]), w_ple_gate=w_ple_gate[0].astype(BF16),
    )
    gfin = row(g_final)

    zk = jnp.zeros((bsz, WINDOW, D_KV), F32)
    zg = jnp.zeros((bsz, CONV_PAD, D_CONV), F32)
    yp, kp, vp, gp = _layer(x_prompt, p_prompt[0], zk, zk, zg, w, gfin,
                            ts=512, tq=128, tb=512, ec=512, lw=256, tm=512, has_past=False)
    ginit = jnp.pad(state_conv[0], ((0, 0), (CONV_PAD - CONV_STATE, 0), (0, 0)))
    ys, ks, vs, gs = _layer(x_sample, p_sample[0],
                            cache_k[0].reshape(dbsz, WINDOW, D_KV), cache_v[0].reshape(dbsz, WINDOW, D_KV),
                            ginit, w, gfin, ts=CHUNK, tq=CHUNK, tb=512, ec=512, lw=256, tm=512, has_past=True)

    kv = lambda a, n: a.reshape(1, n, WINDOW, N_KV_HEADS, HEAD_DIM)
    cv = lambda a: a[None, :, CONV_PAD - CONV_STATE:, :]
    return (yp, ys, kv(kp, bsz), kv(vp, bsz), cv(gp), kv(ks, dbsz), kv(vs, dbsz), cv(gs))
```

```python
import functools

import numpy as np
import jax
import jax.numpy as jnp
from jax import lax
from jax.experimental import pallas as pl
from jax.experimental.pallas import tpu as pltpu

D_MODEL = 1024
D_PLE = 256
D_CONV = 512
CONV_WIDTH = 31
CONV_STATE = CONV_WIDTH - 1
CONV_PAD = 32
N_HEADS = 8
N_KV_HEADS = 2
HEAD_DIM = 64
WINDOW = 128
CHUNK = 64
D_ATTN = N_HEADS * HEAD_DIM
D_KV = N_KV_HEADS * HEAD_DIM
OFF_Q = 2 * D_CONV
OFF_K = OFF_Q + D_ATTN
OFF_V = OFF_K + D_KV
OFF_GATE = OFF_V + D_KV
D_IN = OFF_GATE + 2 * D_MODEL
N_KEYS = 128
N_EXPERTS = N_KEYS * N_KEYS
N_RET_HEADS = 8
D_QUERY = 256
D_HALF = D_QUERY // 2
TOPK = 16
EPS = 1e-6
NEG_INF = -1e30

LANES = 128
SUBLANES = 8
BF16_ROWS = 16
PACKED_KEYS = N_KEYS // 2
VMEM_LIMIT = 56 * 1024 * 1024

BF16 = jnp.bfloat16
F32 = jnp.float32


def _dot(a, b):
    return jnp.dot(a, b, preferred_element_type=F32)


def _dot_nt(a, b):
    return lax.dot_general(a, b, (((1,), (1,)), ((), ())), preferred_element_type=F32)


def _rms(x, g):
    return x * lax.rsqrt(jnp.mean(x * x, axis=-1, keepdims=True) + EPS) * g


def _attn_bias_tables(tq, first):
    band = WINDOW + tq
    i = np.arange(tq)[:, None]
    j = np.arange(band)[None, :]
    dist = np.abs(i + WINDOW - j).astype(np.float32)
    qi, kj = i // CHUNK, j // CHUNK
    valid = (kj >= qi) & (kj <= qi + 2)
    if first:
        valid = valid & (kj >= 2)
    slopes = (2.0 ** (-8.0 * np.arange(1, N_HEADS + 1) / N_HEADS)).astype(np.float32)
    out = np.zeros((4, 2 * tq, band), np.float32)
    for g in range(N_KV_HEADS):
        for par in range(2):
            for half, h in enumerate((4 * g + par, 4 * g + 2 + par)):
                out[2 * g + par, half * tq:(half + 1) * tq] = np.where(valid, -slopes[h] * dist, NEG_INF)
    return out


def _mixer_kernel(sinks_ref, x_ref, kinit_ref, vinit_ref, ginit_ref, biasf_ref, biasg_ref,
                  gmix_ref, win_ref, bin_ref, wdw_ref, bdw_ref, gcln_ref, bcln_ref,
                  wcout_ref, bcout_ref, waout_ref, wout_ref,
                  x1_ref, knew_ref, vnew_ref, gnew_ref,
                  kext, vext, gext, att, *, ts, tq):
    i = pl.program_id(1)
    band = WINDOW + tq

    @pl.when(i == 0)
    def _():
        kext[0:WINDOW, :] = kinit_ref[...]
        vext[0:WINDOW, :] = vinit_ref[...]
        gext[0:CONV_PAD, :] = ginit_ref[...]

    x = x_ref[...]
    hb = _rms(x, gmix_ref[...]).astype(BF16)

    zg = _dot(hb, win_ref[:, 0:OFF_Q]) + bin_ref[:, 0:OFF_Q]
    glu = zg[:, 0:D_CONV] * jax.nn.sigmoid(zg[:, D_CONV:OFF_Q])
    gext[CONV_PAD:CONV_PAD + ts, :] = glu
    conv = jnp.broadcast_to(bdw_ref[...], (ts, D_CONV))
    for j in range(CONV_WIDTH):
        conv = conv + wdw_ref[j:j + 1, :] * gext[CONV_PAD - CONV_STATE + j:CONV_PAD - CONV_STATE + j + ts, :]
    tail = gext[ts:ts + CONV_PAD, :]
    gnew_ref[...] = tail
    gext[0:CONV_PAD, :] = tail
    mu = jnp.mean(conv, axis=-1, keepdims=True)
    cen = conv - mu
    var = jnp.mean(cen * cen, axis=-1, keepdims=True)
    c = cen * lax.rsqrt(var + EPS) * gcln_ref[...] + bcln_ref[...]
    c = c * jax.nn.sigmoid(c)
    branch_a = _dot(c.astype(BF16), wcout_ref[...]) + bcout_ref[...]

    zqkv = _dot(hb, win_ref[:, OFF_Q:OFF_GATE]) + bin_ref[:, OFF_Q:OFF_GATE]
    kext[WINDOW:WINDOW + ts, :] = zqkv[:, D_ATTN:D_ATTN + D_KV]
    vext[WINDOW:WINDOW + ts, :] = zqkv[:, D_ATTN + D_KV:D_ATTN + 2 * D_KV]
    lane = lax.broadcasted_iota(jnp.int32, (band, D_KV), 1)
    lo = lane < HEAD_DIM
    row = lax.broadcasted_iota(jnp.int32, (2 * tq, 1), 0)
    top = row < tq
    for qb in range(ts // tq):
        kb = kext[qb * tq:qb * tq + band, :]
        vb = vext[qb * tq:qb * tq + band, :]
        kr = pltpu.roll(kb, HEAD_DIM, 1)
        vr = pltpu.roll(vb, HEAD_DIM, 1)
        for g in range(N_KV_HEADS):
            ksrc = (kb, kr) if g == 0 else (kr, kb)
            vsrc = (vb, vr) if g == 0 else (vr, vb)
            qst = jnp.concatenate(
                [zqkv[qb * tq:(qb + 1) * tq, 256 * g:256 * g + LANES],
                 zqkv[qb * tq:(qb + 1) * tq, 256 * g + LANES:256 * g + 2 * LANES]], axis=0).astype(BF16)
            o = None
            for par in range(2):
                keep = lo if par == 0 else jnp.logical_not(lo)
                kp = jnp.where(keep, ksrc[par], 0.0).astype(BF16)
                vp = jnp.where(keep, vsrc[par], 0.0).astype(BF16)
                if qb == 0:
                    bias = jnp.where(i == 0, biasf_ref[2 * g + par], biasg_ref[2 * g + par])
                else:
                    bias = biasg_ref[2 * g + par]
                s = _dot_nt(qst, kp) * (HEAD_DIM ** -0.5) + bias
                sink = jnp.where(top, sinks_ref[4 * g + par], sinks_ref[4 * g + 2 + par])
                mx = jnp.maximum(jnp.max(s, axis=-1, keepdims=True), sink)
                p = jnp.exp(s - mx)
                den = jnp.sum(p, axis=-1, keepdims=True) + jnp.exp(sink - mx)
                p = (p / den).astype(BF16)
                pv = _dot(p, vp)
                o = pv if o is None else o + pv
            att[qb * tq:(qb + 1) * tq, 256 * g:256 * g + LANES] = o[0:tq]
            att[qb * tq:(qb + 1) * tq, 256 * g + LANES:256 * g + 2 * LANES] = o[tq:2 * tq]
    ktail = kext[ts:ts + WINDOW, :]
    vtail = vext[ts:ts + WINDOW, :]
    knew_ref[...] = ktail
    vnew_ref[...] = vtail
    kext[0:WINDOW, :] = ktail
    vext[0:WINDOW, :] = vtail
    branch_b = _dot(att[...].astype(BF16), waout_ref[...])

    zgate = _dot(hb, win_ref[:, OFF_GATE:D_IN]) + bin_ref[:, OFF_GATE:D_IN]
    gates = jax.nn.sigmoid(zgate)
    merged = gates[:, 0:D_MODEL] * branch_a + gates[:, D_MODEL:2 * D_MODEL] * branch_b
    x1_ref[...] = x + _dot(merged.astype(BF16), wout_ref[...])


def _const_spec(shape):
    nd = len(shape)
    return pl.BlockSpec(shape, lambda *_: (0,) * nd)


def _mixer(x, kinit, vinit, ginit, sinks, w, *, ts, tq, has_past):
    bsz, seq, _ = x.shape
    nblk = seq // ts
    band = WINDOW + tq
    biasf = jnp.asarray(_attn_bias_tables(tq, not has_past))
    biasg = jnp.asarray(_attn_bias_tables(tq, False))
    per_b = lambda r, c: pl.BlockSpec((None, r, c), lambda b, i: (b, 0, 0))
    in_specs = [
        pl.BlockSpec(memory_space=pltpu.SMEM),
        pl.BlockSpec((None, ts, D_MODEL), lambda b, i: (b, i, 0)),
        per_b(WINDOW, D_KV), per_b(WINDOW, D_KV), per_b(CONV_PAD, D_CONV),
        _const_spec((4, 2 * tq, band)), _const_spec((4, 2 * tq, band)),
        _const_spec((1, D_MODEL)), _const_spec((D_MODEL, D_IN)), _const_spec((1, D_IN)),
        _const_spec((CONV_PAD, D_CONV)), _const_spec((1, D_CONV)), _const_spec((1, D_CONV)),
        _const_spec((1, D_CONV)), _const_spec((D_CONV, D_MODEL)), _const_spec((1, D_MODEL)),
        _const_spec((D_ATTN, D_MODEL)), _const_spec((D_MODEL, D_MODEL)),
    ]
    out_specs = [
        pl.BlockSpec((None, ts, D_MODEL), lambda b, i: (b, i, 0)),
        per_b(WINDOW, D_KV), per_b(WINDOW, D_KV), per_b(CONV_PAD, D_CONV),
    ]
    out_shape = [
        jax.ShapeDtypeStruct((bsz, seq, D_MODEL), F32),
        jax.ShapeDtypeStruct((bsz, WINDOW, D_KV), F32),
        jax.ShapeDtypeStruct((bsz, WINDOW, D_KV), F32),
        jax.ShapeDtypeStruct((bsz, CONV_PAD, D_CONV), F32),
    ]
    scratch = [
        pltpu.VMEM((WINDOW + ts, D_KV), F32),
        pltpu.VMEM((WINDOW + ts, D_KV), F32),
        pltpu.VMEM((CONV_PAD + ts, D_CONV), F32),
        pltpu.VMEM((ts, D_ATTN), F32),
    ]
    return pl.pallas_call(
        functools.partial(_mixer_kernel, ts=ts, tq=tq),
        grid=(bsz, nblk),
        in_specs=in_specs, out_specs=out_specs, out_shape=out_shape, scratch_shapes=scratch,
        compiler_params=pltpu.CompilerParams(
            dimension_semantics=("arbitrary", "arbitrary"), vmem_limit_bytes=VMEM_LIMIT),
        name=f"mixer_ts{ts}",
    )(sinks, x, kinit, vinit, ginit, biasf, biasg,
      w["g_mix"], w["w_in"], w["b_in"], w["w_dw"], w["b_dw"], w["g_cln"], w["b_cln"],
      w["w_cout"], w["b_cout"], w["w_aout"], w["w_out"])


def _oddeven_merge_sort_pairs(n):
    pairs = []

    def merge(lo, hi, r):
        step = r * 2
        if step < hi - lo:
            merge(lo, hi, step)
            merge(lo + r, hi, step)
            for i in range(lo + r, hi - r, step):
                pairs.append((i, i + r))
        else:
            pairs.append((lo, lo + r))

    def sort(lo, hi):
        if hi - lo >= 1:
            mid = lo + (hi - lo) // 2
            sort(lo, mid)
            sort(mid + 1, hi)
            merge(lo, hi, 1)

    sort(0, n - 1)
    return pairs


def _bitonic_merge_pairs(n):
    pairs = []
    s = n // 2
    while s >= 1:
        for i in range(n):
            if (i // s) % 2 == 0:
                pairs.append((i, i + s))
        s //= 2
    return pairs


_SORT16 = _oddeven_merge_sort_pairs(TOPK)
_BITONIC16 = _bitonic_merge_pairs(TOPK)


def _apply_network(xs, pairs):
    xs = list(xs)
    for a, b in pairs:
        hi = jnp.maximum(xs[a], xs[b])
        lo = jnp.minimum(xs[a], xs[b])
        xs[a], xs[b] = hi, lo
    return xs


def _top16_of_two_sorted(xs, ys, sort_result=True):
    zs = [jnp.maximum(xs[k], ys[TOPK - 1 - k]) for k in range(TOPK)]
    return _apply_network(zs, _BITONIC16) if sort_result else zs


_INNER_PAIRS = [(k1, k2) for k1 in range(1, TOPK) for k2 in range(1, TOPK) if (k1 + 1) * (k2 + 1) <= TOPK]


def _peer_prep_kernel(x1_ref, gffn_ref, wq_ref, sk1_ref, sk2_ref,
                      h2t_ref, rk2_ref, e2_ref, cnt_ref, c1_ref,
                      s1s, s2s, vt1, vt2, v1row, v2row, taurow, zrow, *, tb, lw):
    ngrp = tb // LANES
    h2f = _rms(x1_ref[...], gffn_ref[...])
    h2 = h2f.astype(BF16)
    h2t_ref[...] = h2f.T.astype(BF16)
    q = _dot(h2, wq_ref[...]).astype(BF16)
    sk1 = sk1_ref[...]
    sk2 = sk2_ref[...]
    for r in range(N_RET_HEADS):
        s1s[r] = _dot_nt(sk1, q[:, D_QUERY * r:D_QUERY * r + D_HALF])
        s2s[r] = _dot_nt(sk2, q[:, D_QUERY * r + D_HALF:D_QUERY * (r + 1)])

    def top16_rows(s_ref, vt, vrow, r):
        for lg in range(ngrp):
            tile = s_ref[r, :, lg * LANES:(lg + 1) * LANES]
            xs = [tile[SUBLANES * j:SUBLANES * (j + 1), :] for j in range(N_KEYS // SUBLANES)]
            xs = _apply_network(xs, _SORT16)
            for shift in (4, 2, 1):
                ys = [pltpu.roll(v, shift, 0) for v in xs]
                xs = _top16_of_two_sorted(xs, ys)
            for k in range(TOPK):
                vt[r, k, lg:lg + 1, :] = xs[k][0:1, :]
                vrow[r, k:k + 1, lg * LANES:(lg + 1) * LANES] = xs[k][0:1, :]

    def body(r, carry):
        top16_rows(s1s, vt1, v1row, r)
        top16_rows(s2s, vt2, v2row, r)
        return carry

    lax.fori_loop(0, N_RET_HEADS, body, 0)

    ninf = jnp.full((ngrp, LANES), -jnp.inf, F32)
    for r in range(N_RET_HEADS):
        a = [vt1[r, k] for k in range(TOPK)]
        b = [vt2[r, k] for k in range(TOPK)]
        row0 = [a[0] + b[k] for k in range(TOPK)]
        col0 = [a[k] + b[0] for k in range(1, TOPK)] + [ninf]
        inner = [a[k1] + b[k2] for (k1, k2) in _INNER_PAIRS]
        inner = inner + [ninf] * (2 * TOPK - len(inner))
        ab = _top16_of_two_sorted(row0, col0)
        c0 = _apply_network(inner[0:TOPK], _SORT16)
        c1 = _apply_network(inner[TOPK:2 * TOPK], _SORT16)
        cc = _top16_of_two_sorted(c0, c1)
        best = _top16_of_two_sorted(ab, cc, sort_result=False)
        m = row0[0]
        tau = best[0]
        z = jnp.exp(best[0] - m)
        for k in range(1, TOPK):
            tau = jnp.minimum(tau, best[k])
            z = z + jnp.exp(best[k] - m)
        zinv = 1.0 / z
        for lg in range(ngrp):
            sl = slice(lg * LANES, (lg + 1) * LANES)
            taurow[r:r + 1, sl] = tau[lg:lg + 1, :]
            zrow[r:r + 1, sl] = zinv[lg:lg + 1, :]

    def dense(r, carry):
        for lb in range(tb // lw):
            ls = slice(lb * lw, (lb + 1) * lw)
            m1 = v1row[r, 0:1, ls]
            m2 = v2row[r, 0:1, ls]
            tau = taurow[pl.ds(r, 1), ls]
            zi = zrow[pl.ds(r, 1), ls]
            for rg in range(N_KEYS // BF16_ROWS):
                rs = slice(rg * BF16_ROWS, (rg + 1) * BF16_ROWS)
                s1 = s1s[r, rs, ls]
                s2 = s2s[r, rs, ls]
                rk = jnp.zeros((BF16_ROWS, lw), F32)
                cnt = jnp.zeros((BF16_ROWS, lw), F32)
                for k in range(TOPK):
                    vk = v2row[r, k:k + 1, ls]
                    rk = rk + jnp.where(vk > s2, 1.0, 0.0)
                    cnt = cnt + jnp.where(s1 + vk >= tau, 1.0, 0.0)
                flat = pl.ds(pl.multiple_of(r * PACKED_KEYS + rg * SUBLANES, SUBLANES), SUBLANES)
                rk2_ref[flat, ls] = pltpu.bitcast(rk.astype(BF16), jnp.uint32)
                e2_ref[flat, ls] = pltpu.bitcast(jnp.exp(s2 - m2).astype(BF16), jnp.uint32)
                cnt_ref[r, rs, ls] = cnt
                c1_ref[r, rs, ls] = jnp.exp(s1 - m1) * zi
        return carry

    lax.fori_loop(0, N_RET_HEADS, dense, 0)


def _peer_prep(x1, g_ffn, wq, sk1, sk2, *, tb, lw):
    t = x1.shape[0]
    ngrp = tb // LANES
    sc_spec = pl.BlockSpec((N_RET_HEADS, N_KEYS, tb), lambda i: (0, 0, i))
    sc_f32 = jax.ShapeDtypeStruct((N_RET_HEADS, N_KEYS, t), F32)
    flat_spec = pl.BlockSpec((N_RET_HEADS * PACKED_KEYS, tb), lambda i: (0, i))
    sc_bf16 = jax.ShapeDtypeStruct((N_RET_HEADS * PACKED_KEYS, t), jnp.uint32)
    return pl.pallas_call(
        functools.partial(_peer_prep_kernel, tb=tb, lw=lw),
        grid=(t // tb,),
        in_specs=[
            pl.BlockSpec((tb, D_MODEL), lambda i: (i, 0)),
            _const_spec((1, D_MODEL)),
            _const_spec((D_MODEL, N_RET_HEADS * D_QUERY)),
            _const_spec((N_KEYS, D_HALF)), _const_spec((N_KEYS, D_HALF)),
        ],
        out_specs=[pl.BlockSpec((D_MODEL, tb), lambda i: (0, i)), flat_spec, flat_spec, sc_spec, sc_spec],
        out_shape=[jax.ShapeDtypeStruct((D_MODEL, t), BF16), sc_bf16, sc_bf16, sc_f32, sc_f32],
        scratch_shapes=[
            pltpu.VMEM((N_RET_HEADS, N_KEYS, tb), F32),
            pltpu.VMEM((N_RET_HEADS, N_KEYS, tb), F32),
            pltpu.VMEM((N_RET_HEADS, TOPK, ngrp, LANES), F32),
            pltpu.VMEM((N_RET_HEADS, TOPK, ngrp, LANES), F32),
            pltpu.VMEM((N_RET_HEADS, TOPK, tb), F32),
            pltpu.VMEM((N_RET_HEADS, TOPK, tb), F32),
            pltpu.VMEM((N_RET_HEADS, tb), F32),
            pltpu.VMEM((N_RET_HEADS, tb), F32),
        ],
        compiler_params=pltpu.CompilerParams(
            dimension_semantics=("arbitrary",), vmem_limit_bytes=VMEM_LIMIT),
        name=f"peer_prep_t{t}",
    )(x1, g_ffn, wq, sk1, sk2)


def _peer_dense_kernel(h2t_ref, u_ref, vt_ref, rk2_ref, e2_ref, cnt_ref, c1_ref,
                       out_ref, acc, pbuf, *, tb, ec, lw, nchunk):
    j = pl.program_id(1)
    nrow = ec // N_KEYS
    assert nrow == SUBLANES

    @pl.when(j == 0)
    def _():
        acc[...] = jnp.zeros_like(acc)
        pbuf[ec:2 * ec, :] = jnp.zeros((ec, tb), BF16)

    prev_rows = pl.ds(pl.multiple_of(((j + 1) % 2) * ec, ec), ec)

    @pl.when(j < nchunk)
    def _():
        acc[...] += _dot(vt_ref[...], pbuf[prev_rows, :])
        at = _dot(u_ref[...], h2t_ref[...])
        slot_base = (j % 2) * ec
        i1_base = pl.multiple_of(j * nrow, SUBLANES)
        for il in range(nrow):
            for lb in range(tb // lw):
                ls = slice(lb * lw, (lb + 1) * lw)
                cntb, c1b = [], []
                for r in range(N_RET_HEADS):
                    cnt8 = cnt_ref[r, pl.ds(i1_base, SUBLANES), ls]
                    c18 = c1_ref[r, pl.ds(i1_base, SUBLANES), ls]
                    cntb.append(jnp.broadcast_to(cnt8[il:il + 1, :], (BF16_ROWS, lw)).astype(BF16))
                    c1b.append(jnp.broadcast_to(c18[il:il + 1, :], (BF16_ROWS, lw)).astype(BF16))
                for rg in range(N_KEYS // BF16_ROWS):
                    rs = slice(rg * BF16_ROWS, (rg + 1) * BF16_ROWS)
                    w = None
                    for r in range(N_RET_HEADS):
                        frs = slice(r * PACKED_KEYS + rg * SUBLANES, r * PACKED_KEYS + (rg + 1) * SUBLANES)
                        rk2 = pltpu.bitcast(rk2_ref[frs, ls], BF16)
                        e2 = pltpu.bitcast(e2_ref[frs, ls], BF16)
                        term = jnp.where(rk2 < cntb[r], e2, 0.0) * c1b[r]
                        w = term if w is None else w + term
                    e0 = il * N_KEYS + rg * BF16_ROWS
                    dst = pl.ds(pl.multiple_of(slot_base + e0, BF16_ROWS), BF16_ROWS)
                    pbuf[dst, ls] = (jax.nn.gelu(at[e0:e0 + BF16_ROWS, ls]) * w.astype(F32)).astype(BF16)

    @pl.when(j == nchunk)
    def _():
        acc[...] += _dot(vt_ref[...], pbuf[prev_rows, :])
        out_ref[...] = acc[...].T


def _peer_dense(h2t, u, vt, rk2, e2, cnt, c1, *, tb, ec, lw):
    t = h2t.shape[1]
    nchunk = N_EXPERTS // ec
    sc_spec = pl.BlockSpec((N_RET_HEADS, N_KEYS, tb), lambda i, j: (0, 0, i))
    flat_spec = pl.BlockSpec((N_RET_HEADS * PACKED_KEYS, tb), lambda i, j: (0, i))
    return pl.pallas_call(
        functools.partial(_peer_dense_kernel, tb=tb, ec=ec, lw=lw, nchunk=nchunk),
        grid=(t // tb, nchunk + 1),
        in_specs=[
            pl.BlockSpec((D_MODEL, tb), lambda i, j: (0, i)),
            pl.BlockSpec((ec, D_MODEL), lambda i, j: (jnp.minimum(j, nchunk - 1), 0)),
            pl.BlockSpec((D_MODEL, ec), lambda i, j: (0, jnp.maximum(j - 1, 0))),
            flat_spec, flat_spec, sc_spec, sc_spec,
        ],
        out_specs=pl.BlockSpec((tb, D_MODEL), lambda i, j: (i, 0)),
        out_shape=jax.ShapeDtypeStruct((t, D_MODEL), F32),
        scratch_shapes=[
            pltpu.VMEM((D_MODEL, tb), F32),
            pltpu.VMEM((2 * ec, tb), BF16),
        ],
        compiler_params=pltpu.CompilerParams(
            dimension_semantics=("arbitrary", "arbitrary"), vmem_limit_bytes=VMEM_LIMIT),
        name=f"peer_dense_t{t}",
    )(h2t, u, vt, rk2, e2, cnt, c1)


def _ple_kernel(x1_ref, peer_ref, p_ref, gple_ref, wgate_ref, wple_ref, gfin_ref, y_ref):
    x2 = x1_ref[...] + peer_ref[...]
    gate = jax.nn.sigmoid(_dot(_rms(x2, gple_ref[...]).astype(BF16), wgate_ref[...]))
    x3 = x2 + gate * _dot(p_ref[...].astype(BF16), wple_ref[...])
    y_ref[...] = _rms(x3, gfin_ref[...])


def _ple(x1, peer, p, g_ple, wgate, wple, g_final, *, tm):
    t = x1.shape[0]
    row = lambda c: pl.BlockSpec((tm, c), lambda i: (i, 0))
    return pl.pallas_call(
        _ple_kernel,
        grid=(t // tm,),
        in_specs=[row(D_MODEL), row(D_MODEL), row(D_PLE), _const_spec((1, D_MODEL)),
                  _const_spec((D_MODEL, D_MODEL)), _const_spec((D_PLE, D_MODEL)), _const_spec((1, D_MODEL))],
        out_specs=row(D_MODEL),
        out_shape=jax.ShapeDtypeStruct((t, D_MODEL), F32),
        compiler_params=pltpu.CompilerParams(
            dimension_semantics=("arbitrary",), vmem_limit_bytes=VMEM_LIMIT),
        name=f"ple_t{t}",
    )(x1, peer, p, g_ple, wgate, wple, g_final)


def _layer(x, p, kinit, vinit, ginit, w, g_final, *, ts, tq, tb, ec, lw, tm, has_past):
    bsz, seq, _ = x.shape
    x1, knew, vnew, gnew = _mixer(x, kinit, vinit, ginit, w["sinks"], w, ts=ts, tq=tq, has_past=has_past)
    x1f = x1.reshape(bsz * seq, D_MODEL)
    h2t, rk2, e2, cnt, c1 = _peer_prep(x1f, w["g_ffn"], w["w_query"], w["sub_keys1"], w["sub_keys2"],
                                       tb=tb, lw=min(256, tb))
    peer = _peer_dense(h2t, w["expert_u"], w["expert_vt"], rk2, e2, cnt, c1, tb=tb, ec=ec, lw=lw)
    y = _ple(x1f, peer, p.reshape(bsz * seq, D_PLE), w["g_ple"], w["w_ple_gate"], w["w_ple"], g_final, tm=tm)
    return y.reshape(bsz, seq, D_MODEL), knew, vnew, gnew


def kernel(x_prompt, x_sample, p_prompt, p_sample, cache_k, cache_v, state_conv, g_mix, w_in, b_in, w_dw, b_dw, g_cln, b_cln, w_cout, b_cout, sinks, w_aout, w_out, g_ffn, w_query, sub_keys1, sub_keys2, expert_u, expert_v, w_ple, g_ple, w_ple_gate, g_final):
    depth = g_mix.shape[0]
    assert depth == 1
    bsz, seq, _ = x_prompt.shape
    dbsz, dseq, _ = x_sample.shape
    n_cache = cache_k.shape[2]
    assert n_cache == WINDOW and dseq == CHUNK and seq % 512 == 0

    row = lambda a: a.reshape(1, -1)
    w = dict(
        g_mix=row(g_mix[0]), w_in=w_in[0].astype(BF16), b_in=row(b_in[0]),
        w_dw=jnp.pad(w_dw[0], ((0, CONV_PAD - CONV_WIDTH), (0, 0))), b_dw=row(b_dw[0]),
        g_cln=row(g_cln[0]), b_cln=row(b_cln[0]),
        w_cout=w_cout[0].astype(BF16), b_cout=row(b_cout[0]), sinks=sinks[0],
        w_aout=w_aout[0].astype(BF16), w_out=w_out[0].astype(BF16),
        g_ffn=row(g_ffn[0]), w_query=w_query[0].astype(BF16),
        sub_keys1=sub_keys1[0].astype(BF16), sub_keys2=sub_keys2[0].astype(BF16),
        expert_u=expert_u[0].astype(BF16), expert_vt=expert_v[0].astype(BF16).T,
        w_ple=w_ple[0].astype(BF16), g_ple=row(g_ple[0]), w_ple_gate=w_ple_gate[0].astype(BF16),
    )
    gfin = row(g_final)

    zk = jnp.zeros((bsz, WINDOW, D_KV), F32)
    zg = jnp.zeros((bsz, CONV_PAD, D_CONV), F32)
    yp, kp, vp, gp = _layer(x_prompt, p_prompt[0], zk, zk, zg, w, gfin,
                            ts=512, tq=128, tb=512, ec=1024, lw=128, tm=512, has_past=False)
    ginit = jnp.pad(state_conv[0], ((0, 0), (CONV_PAD - CONV_STATE, 0), (0, 0)))
    ys, ks, vs, gs = _layer(x_sample, p_sample[0],
                            cache_k[0].reshape(dbsz, WINDOW, D_KV), cache_v[0].reshape(dbsz, WINDOW, D_KV),
                            ginit, w, gfin, ts=CHUNK, tq=CHUNK, tb=512, ec=1024, lw=128, tm=512, has_past=True)

    kv = lambda a, n: a.reshape(1, n, WINDOW, N_KV_HEADS, HEAD_DIM)
    cv = lambda a: a[None, :, CONV_PAD - CONV_STATE:, :]
    return (yp, ys, kv(kp, bsz), kv(vp, bsz), cv(gp), kv(ks, dbsz), kv(vs, dbsz), cv(gs))
```
